```python
import math
import jax, jax.numpy as jnp
from jax import lax
import numpy as np

D_MODEL = 2048
BATCH = 2
SEQ = 4096
DEPTH = 4

CHUNK = 64
ROPE_BASE = 10000.0
N_A = DEPTH // 2
N_B = DEPTH - N_A

ALPHA = (2 * DEPTH) ** 0.25
BETA = (8 * DEPTH) ** -0.25

RET_HEADS = D_MODEL // 256
RET_QK_DIM = 256
RET_V_DIM = 512
RET_QK = RET_HEADS * RET_QK_DIM
RET_V = RET_HEADS * RET_V_DIM
RET_IN = 2 * RET_QK + 2 * RET_V

MLA_HEADS = D_MODEL // 128
MLA_NOPE = 128
MLA_ROPE = 64
MLA_V = 128
MLA_KV_RANK = 512
MLA_Q_RANK = 512
MLA_Q_BLOCK = 128

PEER_HEADS = 8
PEER_NKEYS = 128
PEER_EXPERTS = PEER_NKEYS * PEER_NKEYS
PEER_QDIM = 256
PEER_HALF = PEER_QDIM // 2
PEER_TOPK = 16
PEER_TOKEN_BLOCK = 128

NORM_EPS = 1e-5

kernel_name = "yoco_retention_mla_peer_deepnorm"


def layer_norm(x, g, b):
    x32 = x.astype(jnp.float32)
    mu = jnp.mean(x32, axis=-1, keepdims=True)
    var = jnp.mean(jnp.square(x32 - mu), axis=-1, keepdims=True)
    y = (x32 - mu) * lax.rsqrt(var + NORM_EPS)
    return (y * g.astype(jnp.float32) + b.astype(jnp.float32)).astype(x.dtype)


def rms_norm(x, g):
    x32 = x.astype(jnp.float32)
    y = x32 * lax.rsqrt(jnp.mean(jnp.square(x32), axis=-1, keepdims=True) + NORM_EPS)
    return (y * g.astype(jnp.float32)).astype(x.dtype)


def rope(x, positions):
    half = x.shape[-1] // 2
    freq = ROPE_BASE ** (-jnp.arange(half, dtype=jnp.float32) / half)
    ang = positions.astype(jnp.float32)[..., None] * freq
    cos = jnp.cos(ang)[:, :, None, :].astype(x.dtype)
    sin = jnp.sin(ang)[:, :, None, :].astype(x.dtype)
    x1, x2 = x[..., :half], x[..., half:]
    return jnp.concatenate([x1 * cos - x2 * sin, x2 * cos + x1 * sin], axis=-1)


def retention(x, positions, w_in, w_out):
    B, S, _ = x.shape
    NC = S // CHUNK
    proj = x @ w_in
    q = proj[..., :RET_QK].reshape(B, S, RET_HEADS, RET_QK_DIM)
    k = proj[..., RET_QK:2 * RET_QK].reshape(B, S, RET_HEADS, RET_QK_DIM)
    v = proj[..., 2 * RET_QK:2 * RET_QK + RET_V].reshape(B, S, RET_HEADS, RET_V_DIM)
    gate = proj[..., 2 * RET_QK + RET_V:]
    q = rope(q, positions)
    k = rope(k, positions) * (RET_QK_DIM ** -0.5)

    def to_chunks(t):
        return t.reshape(B, NC, CHUNK, RET_HEADS, -1).transpose(1, 0, 3, 2, 4)

    qc, kc, vc = to_chunks(q), to_chunks(k), to_chunks(v)

    h_idx = jnp.arange(RET_HEADS, dtype=jnp.float32)
    log_gamma = jnp.log(1.0 - jnp.exp2(-5.0 - h_idx))
    n = jnp.arange(CHUNK, dtype=jnp.float32)
    dt = q.dtype
    intra = jnp.exp(log_gamma[:, None, None] * jnp.abs(n[:, None] - n[None, :])).astype(dt)
    q_dec = jnp.exp(log_gamma[:, None] * (n + 1.0)).astype(dt)
    k_dec = jnp.exp(log_gamma[:, None] * (CHUNK - 1.0 - n)).astype(dt)
    c_dec = jnp.exp(log_gamma * CHUNK).astype(dt)

    def step(state, inp):
        qb, kb, vb = inp
        scores = jnp.einsum('bhnd,bhmd->bhnm', qb, kb) * intra
        inner = jnp.einsum('bhnm,bhmv->bhnv', scores, vb)
        cross = jnp.einsum('bhnd,bhdv->bhnv', qb, state) * q_dec[None, :, :, None]
        new_state = state * c_dec[None, :, None, None] + jnp.einsum(
            'bhmd,bhmv->bhdv', kb * k_dec[None, :, :, None], vb)
        return new_state.astype(state.dtype), (inner + cross).astype(dt)

    state0 = jnp.zeros((B, RET_HEADS, RET_QK_DIM, RET_V_DIM), dt)
    _, o = lax.scan(step, state0, (qc, kc, vc))
    o = o.transpose(1, 0, 3, 2, 4).reshape(B, S, RET_HEADS, RET_V_DIM)
    o32 = o.astype(jnp.float32)
    mu = jnp.mean(o32, axis=-1, keepdims=True)
    var = jnp.mean(jnp.square(o32 - mu), axis=-1, keepdims=True)
    o = ((o32 - mu) * lax.rsqrt(var + NORM_EPS)).astype(dt).reshape(B, S, RET_V)
    return (jax.nn.silu(gate) * o) @ w_out


def mla_shared_kv(h, positions, w_dkv, kv_norm, w_ukv):
    B, S, _ = h.shape
    ckr = h @ w_dkv
    c = rms_norm(ckr[..., :MLA_KV_RANK], kv_norm)
    k_rope = rope(ckr[..., MLA_KV_RANK:][:, :, None, :], positions)[:, :, 0, :]
    kv = (c @ w_ukv).reshape(B, S, MLA_HEADS, MLA_NOPE + MLA_V)
    return kv[..., :MLA_NOPE], k_rope, kv[..., MLA_NOPE:]


def mla_attention(x, positions, k_nope, k_rope, v, w_dq, q_norm, w_uq, w_o):
    B, S, _ = x.shape
    q = (rms_norm(x @ w_dq, q_norm) @ w_uq).reshape(B, S, MLA_HEADS, MLA_NOPE + MLA_ROPE)
    q_nope = q[..., :MLA_NOPE]
    q_rope = rope(q[..., MLA_NOPE:], positions)
    scale = (MLA_NOPE + MLA_ROPE) ** -0.5
    NQB = S // MLA_Q_BLOCK
    qn_b = q_nope.reshape(B, NQB, MLA_Q_BLOCK, MLA_HEADS, MLA_NOPE).transpose(1, 0, 2, 3, 4)
    qr_b = q_rope.reshape(B, NQB, MLA_Q_BLOCK, MLA_HEADS, MLA_ROPE).transpose(1, 0, 2, 3, 4)
    key_chunk = jnp.arange(S) // CHUNK

    def block(args):
        qn, qr, j = args
        s = jnp.einsum('bqhd,bkhd->bhqk', qn, k_nope) + jnp.einsum('bqhd,bkd->bhqk', qr, k_rope)
        s = s.astype(jnp.float32) * scale
        q_chunk = (j * MLA_Q_BLOCK + jnp.arange(MLA_Q_BLOCK)) // CHUNK
        mask = key_chunk[None, :] <= q_chunk[:, None]
        p = jax.nn.softmax(jnp.where(mask[None, None], s, -jnp.inf), axis=-1).astype(v.dtype)
        return jnp.einsum('bhqk,bkhd->bqhd', p, v)

    o = lax.map(block, (qn_b, qr_b, jnp.arange(NQB, dtype=jnp.int32)))
    o = o.transpose(1, 0, 2, 3, 4).reshape(B, S, MLA_HEADS * MLA_V)
    return o @ w_o


def peer(x, w_q, sub_keys, u_tab, v_tab):
    B, S, D = x.shape
    T = B * S
    xt = x.reshape(T, D)
    q = (xt @ w_q).reshape(T, PEER_HEADS, 2, PEER_HALF)
    s = jnp.einsum('thpd,hpkd->thpk', q, sub_keys).astype(jnp.float32)
    top_v, top_i = lax.top_k(s, PEER_TOPK)
    cand = (top_v[:, :, 0, :, None] + top_v[:, :, 1, None, :]).reshape(T, PEER_HEADS, PEER_TOPK * PEER_TOPK)
    g_val, c_idx = lax.top_k(cand, PEER_TOPK)
    i1 = jnp.take_along_axis(top_i[:, :, 0], c_idx // PEER_TOPK, axis=-1)
    i2 = jnp.take_along_axis(top_i[:, :, 1], c_idx % PEER_TOPK, axis=-1)
    experts = i1 * PEER_NKEYS + i2
    gates = jax.nn.softmax(g_val, axis=-1).astype(x.dtype)

    nb = T // PEER_TOKEN_BLOCK
    xb = xt.reshape(nb, PEER_TOKEN_BLOCK, D)
    eb = experts.reshape(nb, PEER_TOKEN_BLOCK, PEER_HEADS, PEER_TOPK)
    gb = gates.reshape(nb, PEER_TOKEN_BLOCK, PEER_HEADS, PEER_TOPK)

    def block(args):
        xk, ek, gk = args
        u = u_tab[ek]
        act = jax.nn.gelu(jnp.einsum('thkd,td->thk', u, xk))
        return jnp.einsum('thk,thkd->td', gk * act, v_tab[ek])

    return lax.map(block, (xb, eb, gb)).reshape(B, S, D)


def setup_inputs(seed: int = 0) -> dict:
    key = jax.random.key(seed)
    ks = jax.random.split(key, 24)
    f32 = jnp.float32
    nrm = lambda k, shape, scale: jax.random.normal(k, shape, f32) * scale
    D = D_MODEL

    x = jax.random.normal(ks[0], (BATCH, SEQ, D), f32)
    offset = jax.random.randint(ks[1], (BATCH,), 0, 1024) * CHUNK
    positions = (offset[:, None] + jnp.arange(SEQ)[None, :]).astype(jnp.int32)

    ret_w_in = jnp.concatenate([
        nrm(ks[2], (N_A, D, 2 * RET_QK), D ** -0.5),
        nrm(ks[3], (N_A, D, RET_V), BETA * D ** -0.5),
        nrm(ks[4], (N_A, D, RET_V), D ** -0.5)], axis=-1)
    ret_w_out = nrm(ks[5], (N_A, RET_V, D), BETA * RET_V ** -0.5)

    mla_w_dkv = nrm(ks[6], (D, MLA_KV_RANK + MLA_ROPE), D ** -0.5)
    mla_kv_norm = 1.0 + nrm(ks[7], (MLA_KV_RANK,), 0.02)
    mla_w_ukv = jnp.concatenate([
        nrm(ks[8], (MLA_KV_RANK, MLA_HEADS, MLA_NOPE), MLA_KV_RANK ** -0.5),
        nrm(ks[9], (MLA_KV_RANK, MLA_HEADS, MLA_V), BETA * MLA_KV_RANK ** -0.5)],
        axis=-1).reshape(MLA_KV_RANK, MLA_HEADS * (MLA_NOPE + MLA_V))
    mla_w_dq = nrm(ks[10], (N_B, D, MLA_Q_RANK), D ** -0.5)
    mla_q_norm = 1.0 + nrm(ks[11], (N_B, MLA_Q_RANK), 0.02)
    mla_w_uq = nrm(ks[12], (N_B, MLA_Q_RANK, MLA_HEADS * (MLA_NOPE + MLA_ROPE)), MLA_Q_RANK ** -0.5)
    mla_w_o = nrm(ks[13], (N_B, MLA_HEADS * MLA_V, D), BETA * (MLA_HEADS * MLA_V) ** -0.5)

    peer_w_q = nrm(ks[14], (DEPTH, D, PEER_HEADS * PEER_QDIM), D ** -0.5)
    peer_sub_keys = nrm(ks[15], (DEPTH, PEER_HEADS, 2, PEER_NKEYS, PEER_HALF), PEER_HALF ** -0.5)
    peer_u = nrm(ks[16], (DEPTH, PEER_EXPERTS, D), D ** -0.5)
    peer_v = nrm(ks[17], (DEPTH, PEER_EXPERTS, D), BETA * PEER_HEADS ** -0.5)

    ln_g = 1.0 + nrm(ks[18], (DEPTH, 2, D), 0.02)
    ln_b = nrm(ks[19], (DEPTH, 2, D), 0.02)

    return {"x": x, "positions": positions,
            "ret_w_in": ret_w_in, "ret_w_out": ret_w_out,
            "mla_w_dkv": mla_w_dkv, "mla_kv_norm": mla_kv_norm, "mla_w_ukv": mla_w_ukv,
            "mla_w_dq": mla_w_dq, "mla_q_norm": mla_q_norm, "mla_w_uq": mla_w_uq, "mla_w_o": mla_w_o,
            "peer_w_q": peer_w_q, "peer_sub_keys": peer_sub_keys, "peer_u": peer_u, "peer_v": peer_v,
            "ln_g": ln_g, "ln_b": ln_b}


def reference(x, positions, ret_w_in, ret_w_out, mla_w_dkv, mla_kv_norm, mla_w_ukv,
              mla_w_dq, mla_q_norm, mla_w_uq, mla_w_o, peer_w_q, peer_sub_keys,
              peer_u, peer_v, ln_g, ln_b):
    shared = None
    for l in range(DEPTH):
        if l < N_A:
            mix = retention(x, positions, ret_w_in[l], ret_w_out[l])
        else:
            if shared is None:
                shared = mla_shared_kv(x, positions, mla_w_dkv, mla_kv_norm, mla_w_ukv)
            j = l - N_A
            mix = mla_attention(x, positions, shared[0], shared[1], shared[2],
                                mla_w_dq[j], mla_q_norm[j], mla_w_uq[j], mla_w_o[j])
        x = layer_norm(ALPHA * x + mix, ln_g[l, 0], ln_b[l, 0])
        ffn = peer(x, peer_w_q[l], peer_sub_keys[l], peer_u[l], peer_v[l])
        x = layer_norm(ALPHA * x + ffn, ln_g[l, 1], ln_b[l, 1])
    return x
```

```python
import functools
import math

import jax
import jax.numpy as jnp
from jax import lax
from jax.experimental import pallas as pl
from jax.experimental.pallas import tpu as pltpu

F32 = jnp.float32
BF16 = jnp.bfloat16

D_MODEL = 2048
DEPTH = 4
CHUNK = 64
ROPE_BASE = 10000.0
N_A = DEPTH // 2
ALPHA = (2 * DEPTH) ** 0.25
NORM_EPS = 1e-5

RET_HEADS = 8
RET_QK_DIM = 256
RET_V_DIM = 512
RET_QK = RET_HEADS * RET_QK_DIM
RET_V = RET_HEADS * RET_V_DIM
RET_SUPER = 256

MLA_HEADS = 16
MLA_NOPE = 128
MLA_ROPE = 64
MLA_V = 128
MLA_KV_RANK = 512
MLA_Q_RANK = 512
MLA_QK_PAD = 256
MLA_TQ = 512
MLA_TK = 512

PEER_HEADS = 8
PEER_NKEYS = 128
PEER_EXPERTS = PEER_NKEYS * PEER_NKEYS
PEER_TOPK = 16
PEER_TM = 512
PEER_EB = 512
PEER_SUB = 256

LANES = 128
VMEM_LIMIT = 56 * 1024 * 1024
LOG2E = 1.4426950408889634


def _chunk_of(idx):
    return lax.shift_right_logical(idx, int(math.log2(CHUNK)))


def _params(*sem):
    return pltpu.CompilerParams(dimension_semantics=sem, vmem_limit_bytes=VMEM_LIMIT)


def _rope_tables_body(pos_ref, fr_ref, fm_ref, cr_ref, sr_ref, cm_ref, s1_ref, s2_ref):
    pos = pos_ref[...]
    ang = pos * fr_ref[...]
    cr_ref[...] = jnp.cos(ang)
    sr_ref[...] = jnp.sin(ang)
    angm = pos * fm_ref[...]
    lane = lax.broadcasted_iota(jnp.int32, angm.shape, 1)
    half = MLA_ROPE // 2
    c = jnp.cos(angm)
    s = jnp.sin(angm)
    cm_ref[...] = jnp.where(lane < MLA_ROPE, c, 0.0)
    s1_ref[...] = jnp.where(lane < half, -s, 0.0)
    s2_ref[...] = jnp.where((lane >= half) & (lane < MLA_ROPE), s, 0.0)


def _rope_tables(pos_b):
    T = pos_b.shape[0]
    tm = 1024
    half_r = RET_QK_DIM // 2
    fr = ROPE_BASE ** (-jnp.arange(half_r, dtype=F32) / half_r)
    half_m = MLA_ROPE // 2
    fm = ROPE_BASE ** (-jnp.arange(half_m, dtype=F32) / half_m)
    fm = jnp.concatenate([fm, fm, jnp.zeros((LANES - MLA_ROPE,), F32)])
    row = pl.BlockSpec((tm, LANES), lambda i: (i, 0))
    one = pl.BlockSpec((1, LANES), lambda i: (0, 0))
    sds = jax.ShapeDtypeStruct((T, LANES), F32)
    return pl.pallas_call(
        _rope_tables_body,
        grid=(T // tm,),
        in_specs=[row, one, one],
        out_specs=[row] * 5,
        out_shape=[sds] * 5,
        compiler_params=_params("arbitrary"),
        name="rope_tables",
    )(pos_b, fr.reshape(1, LANES), fm.reshape(1, LANES))


def _rope_small(r, c, s1, s2):
    return r * c + pltpu.roll(r, 96, 1) * s1 + pltpu.roll(r, 32, 1) * s2


def _mm_body(*refs, n_extra, epilogue):
    a_ref, w_ref = refs[:2]
    extras = refs[2:2 + n_extra]
    outs = refs[2 + n_extra:]
    acc = jnp.dot(a_ref[...], w_ref[...], preferred_element_type=F32)
    epilogue(acc, extras, outs)


def _matmul(a, w, *, tm, tn, n_cols, w_col_off, epilogue, extras, outs, name):
    M, K = a.shape
    off = w_col_off // tn
    in_specs = [pl.BlockSpec((tm, K), lambda i, j: (i, 0)),
                pl.BlockSpec((K, tn), lambda i, j: (0, j + off))]
    in_specs += [s for _, s in extras]
    return pl.pallas_call(
        functools.partial(_mm_body, n_extra=len(extras), epilogue=epilogue),
        grid=(M // tm, n_cols // tn),
        in_specs=in_specs,
        out_specs=[s for _, s in outs],
        out_shape=[s for s, _ in outs],
        compiler_params=_params("parallel", "arbitrary"),
        name=name,
    )(a, w, *[x for x, _ in extras])


def _epi_plain(acc, extras, outs):
    outs[0][...] = acc.astype(outs[0].dtype)


def _epi_ret_rope(acc, extras, outs):
    cos = extras[0][...]
    sin = extras[1][...]
    j = pl.program_id(1)
    scale = jnp.where(j * acc.shape[1] >= RET_QK, RET_QK_DIM ** -0.5, 1.0).astype(F32)
    half = RET_QK_DIM // 2
    for hh in range(acc.shape[1] // RET_QK_DIM):
        s0 = hh * RET_QK_DIM
        x1 = acc[:, s0:s0 + half]
        x2 = acc[:, s0 + half:s0 + RET_QK_DIM]
        outs[0][:, s0:s0 + half] = ((x1 * cos - x2 * sin) * scale).astype(outs[0].dtype)
        outs[0][:, s0 + half:s0 + RET_QK_DIM] = ((x2 * cos + x1 * sin) * scale).astype(outs[0].dtype)


def _rms(acc, g):
    ms = jnp.mean(acc * acc, axis=-1, keepdims=True)
    return acc * lax.rsqrt(ms + NORM_EPS) * g


def _epi_rms(acc, extras, outs):
    outs[0][...] = _rms(acc, extras[0][...]).astype(outs[0].dtype)


def _epi_dkv(acc, extras, outs):
    g, c, s1, s2 = (e[...] for e in extras)
    outs[0][...] = _rms(acc[:, :MLA_KV_RANK], g).astype(outs[0].dtype)
    r = acc[:, MLA_KV_RANK:MLA_KV_RANK + LANES]
    outs[1][...] = _rope_small(r, c, s1, s2).astype(outs[1].dtype)


def _epi_mla_q(acc, extras, outs):
    c, s1, s2 = (e[...] for e in extras)
    for hh in range(acc.shape[1] // MLA_QK_PAD):
        s0 = hh * MLA_QK_PAD
        outs[0][:, s0:s0 + MLA_NOPE] = acc[:, s0:s0 + MLA_NOPE].astype(outs[0].dtype)
        r = acc[:, s0 + MLA_NOPE:s0 + MLA_QK_PAD]
        outs[0][:, s0 + MLA_NOPE:s0 + MLA_QK_PAD] = _rope_small(r, c, s1, s2).astype(outs[0].dtype)


def _epi_mla_k(acc, extras, outs):
    kr = extras[0][...]
    for hh in range(acc.shape[1] // MLA_NOPE):
        s0 = hh * MLA_QK_PAD
        outs[0][:, s0:s0 + MLA_NOPE] = acc[:, hh * MLA_NOPE:(hh + 1) * MLA_NOPE].astype(outs[0].dtype)
        outs[0][:, s0 + MLA_NOPE:s0 + MLA_QK_PAD] = kr


def _ln(z, g, b):
    mu = jnp.mean(z, axis=-1, keepdims=True)
    zc = z - mu
    var = jnp.mean(zc * zc, axis=-1, keepdims=True)
    return zc * lax.rsqrt(var + NORM_EPS) * g + b


def _mm_res_ln_body(a_ref, w_ref, x_ref, g_ref, b_ref, of_ref, ob_ref, acc_ref, *, nk):
    k = pl.program_id(1)

    @pl.when(k == 0)
    def _():
        acc_ref[...] = jnp.zeros_like(acc_ref)

    acc_ref[...] += jnp.dot(a_ref[...], w_ref[...], preferred_element_type=F32)

    @pl.when(k == nk - 1)
    def _():
        y = _ln(ALPHA * x_ref[...] + acc_ref[...], g_ref[...], b_ref[...])
        of_ref[...] = y
        ob_ref[...] = y.astype(BF16)


def _mm_res_ln(a, w, x, g, b, *, tm=512, tk=1024, name):
    M, K = a.shape
    N = w.shape[1]
    tk = min(tk, K)
    nk = K // tk
    row = pl.BlockSpec((tm, N), lambda i, k: (i, 0))
    vec = pl.BlockSpec((1, N), lambda i, k: (0, 0))
    return pl.pallas_call(
        functools.partial(_mm_res_ln_body, nk=nk),
        grid=(M // tm, nk),
        in_specs=[pl.BlockSpec((tm, tk), lambda i, k: (i, k)),
                  pl.BlockSpec((tk, N), lambda i, k: (k, 0)),
                  row, vec, vec],
        out_specs=[row, row],
        out_shape=[jax.ShapeDtypeStruct((M, N), F32), jax.ShapeDtypeStruct((M, N), BF16)],
        scratch_shapes=[pltpu.VMEM((tm, N), F32)],
        compiler_params=_params("parallel", "arbitrary"),
        name=name,
    )(a, w, x, g, b)


def _res_ln_t_body(ft_ref, x_ref, g_ref, b_ref, of_ref, ob_ref):
    y = _ln(ALPHA * x_ref[...] + ft_ref[...].T, g_ref[...], b_ref[...])
    of_ref[...] = y
    ob_ref[...] = y.astype(BF16)


def _res_ln_t(ft, x, g, b, *, tm=512):
    M, N = x.shape
    row = pl.BlockSpec((tm, N), lambda i: (i, 0))
    vec = pl.BlockSpec((1, N), lambda i: (0, 0))
    return pl.pallas_call(
        _res_ln_t_body,
        grid=(M // tm,),
        in_specs=[pl.BlockSpec((N, tm), lambda i: (0, i)), row, vec, vec],
        out_specs=[row, row],
        out_shape=[jax.ShapeDtypeStruct((M, N), F32), jax.ShapeDtypeStruct((M, N), BF16)],
        compiler_params=_params("parallel"),
        name="peer_res_ln",
    )(ft, x, g, b)


def _ret_body(lg_ref, q_ref, k_ref, v_ref, gate_ref, o_ref, state_ref):
    h = pl.program_id(1)
    c = pl.program_id(2)
    sc = RET_SUPER

    @pl.when(c == 0)
    def _():
        state_ref[...] = jnp.zeros_like(state_ref)

    lg = lg_ref[h]
    n = lax.broadcasted_iota(jnp.int32, (sc, sc), 0)
    m = lax.broadcasted_iota(jnp.int32, (sc, sc), 1)
    cn = _chunk_of(n)
    cm = _chunk_of(m)
    dist = jnp.where(cn == cm, jnp.abs(n - m), n - m).astype(F32)
    dmat = jnp.where(cm <= cn, jnp.exp(lg * dist), 0.0)
    ncol = lax.broadcasted_iota(jnp.int32, (sc, 1), 0).astype(F32)
    q_dec = jnp.exp(lg * (ncol + 1.0))
    k_dec = jnp.exp(lg * (sc - 1.0 - ncol))
    c_dec = jnp.exp(jnp.full((1, RET_V_DIM), lg * sc, F32))

    q = q_ref[...]
    k = k_ref[...]
    v = v_ref[...]
    state = state_ref[...]
    scores = lax.dot_general(q, k, (((1,), (1,)), ((), ())), preferred_element_type=F32) * dmat
    inner = jnp.dot(scores.astype(BF16), v, preferred_element_type=F32)
    cross = jnp.dot(q, state.astype(BF16), preferred_element_type=F32) * q_dec
    kd = (k.astype(F32) * k_dec).astype(BF16)
    state_ref[...] = state * c_dec + lax.dot_general(
        kd, v, (((0,), (0,)), ((), ())), preferred_element_type=F32)

    o = inner + cross
    mu = jnp.mean(o, axis=-1, keepdims=True)
    oc = o - mu
    var = jnp.mean(oc * oc, axis=-1, keepdims=True)
    on = oc * lax.rsqrt(var + NORM_EPS)
    g = gate_ref[...].astype(F32)
    o_ref[...] = (g * jax.nn.sigmoid(g) * on).astype(o_ref.dtype)


def _retention_core(qk, vg, log_gamma, batch, seq):
    T = qk.shape[0]
    sc = RET_SUPER
    nsc = seq // sc
    H = RET_HEADS
    row = lambda b, h, c: b * nsc + c
    return pl.pallas_call(
        _ret_body,
        grid=(batch, H, nsc),
        in_specs=[pl.BlockSpec(memory_space=pltpu.SMEM),
                  pl.BlockSpec((sc, RET_QK_DIM), lambda b, h, c: (row(b, h, c), h)),
                  pl.BlockSpec((sc, RET_QK_DIM), lambda b, h, c: (row(b, h, c), H + h)),
                  pl.BlockSpec((sc, RET_V_DIM), lambda b, h, c: (row(b, h, c), h)),
                  pl.BlockSpec((sc, RET_V_DIM), lambda b, h, c: (row(b, h, c), H + h))],
        out_specs=pl.BlockSpec((sc, RET_V_DIM), lambda b, h, c: (row(b, h, c), h)),
        out_shape=jax.ShapeDtypeStruct((T, RET_V), BF16),
        scratch_shapes=[pltpu.VMEM((RET_QK_DIM, RET_V_DIM), F32)],
        compiler_params=_params("parallel", "parallel", "arbitrary"),
        name="retention_core",
    )(log_gamma, qk, qk, vg, vg)


def _mla_attn_body(q_ref, k_ref, v_ref, o_ref, *, scale):
    i = pl.program_id(2)
    tq, tk = MLA_TQ, MLA_TK
    q = q_ref[...]

    def block(j, carry, masked):
        m_prev, l_prev, acc = carry
        k = k_ref[pl.ds(pl.multiple_of(j * tk, tk), tk), :]
        v = v_ref[pl.ds(pl.multiple_of(j * tk, tk), tk), :]
        s = lax.dot_general(q, k, (((1,), (1,)), ((), ())), preferred_element_type=F32) * scale
        if masked:
            qc = _chunk_of(lax.broadcasted_iota(jnp.int32, (tq, tk), 0))
            kc = _chunk_of(lax.broadcasted_iota(jnp.int32, (tq, tk), 1))
            s = jnp.where(kc <= qc, s, -jnp.inf)
        m_new = jnp.maximum(m_prev, jnp.max(s, axis=-1, keepdims=True))
        a = jnp.exp(m_prev - m_new)
        p = jnp.exp(s - m_new)
        l_new = a * l_prev + jnp.sum(p, axis=-1, keepdims=True)
        acc = a * acc + jnp.dot(p.astype(BF16), v, preferred_element_type=F32)
        return m_new, l_new, acc

    init = (jnp.full((tq, 1), -jnp.inf, F32), jnp.zeros((tq, 1), F32), jnp.zeros((tq, MLA_V), F32))
    carry = lax.fori_loop(0, i, lambda j, c: block(j, c, False), init)
    m, l, acc = block(i, carry, True)
    o_ref[...] = (acc / l).astype(o_ref.dtype)


def _mla_attention(qf, kf, v, batch, seq):
    T = qf.shape[0]
    assert MLA_TQ == MLA_TK and MLA_TQ % CHUNK == 0
    nq = seq // MLA_TQ
    scale = (MLA_NOPE + MLA_ROPE) ** -0.5
    return pl.pallas_call(
        functools.partial(_mla_attn_body, scale=scale),
        grid=(batch, MLA_HEADS, nq),
        in_specs=[pl.BlockSpec((MLA_TQ, MLA_QK_PAD), lambda b, h, i: (b * nq + i, h)),
                  pl.BlockSpec((seq, MLA_QK_PAD), lambda b, h, i: (b, h)),
                  pl.BlockSpec((seq, MLA_V), lambda b, h, i: (b, h))],
        out_specs=pl.BlockSpec((MLA_TQ, MLA_V), lambda b, h, i: (b * nq + i, h)),
        out_shape=jax.ShapeDtypeStruct((T, MLA_HEADS * MLA_V), BF16),
        compiler_params=_params("parallel", "parallel", "arbitrary"),
        name="mla_attention",
    )(qf, kf, v)


_PEER_PAIRS = [(i, j) for i in range(PEER_TOPK) for j in range(PEER_TOPK)
               if (i + 1) * (j + 1) <= PEER_TOPK]


def _peer_scores_body(q_ref, sk_ref, s1_ref, s1l_ref, s2_ref, s2l_ref, tau_ref, top_ref):
    H = PEER_HEADS
    neg = -jnp.inf
    for h in range(H):
        for p in range(2):
            q = q_ref[:, (2 * h + p) * LANES:(2 * h + p + 1) * LANES]
            st = lax.dot_general(sk_ref[2 * h + p], q, (((1,), (1,)), ((), ())),
                                 preferred_element_type=F32)
            (s1_ref if p == 0 else s2_ref)[h] = st
            vals = st
            for r in range(PEER_TOPK):
                mx = jnp.max(vals, axis=0, keepdims=True)
                top_ref[p, r, h:h + 1, :] = mx
                vals = jnp.where(vals == mx, neg, vals)

    cands = [top_ref[0, i] + top_ref[1, j] for i, j in _PEER_PAIRS]
    tau = jnp.full_like(cands[0], neg)
    for ci in cands:
        cnt = jnp.zeros_like(ci)
        for cj in cands:
            cnt = cnt + jnp.where(cj >= ci, 1.0, 0.0)
        tau = jnp.maximum(tau, jnp.where(cnt >= float(PEER_TOPK), ci, neg))
    mx = cands[0]
    z = jnp.zeros_like(mx)
    for ci in cands:
        z = z + jnp.where(ci >= tau, jnp.exp(ci - mx), 0.0)
    off = mx + jnp.log(z)
    tau_ref[...] = tau
    for h in range(H):
        s1l_ref[h] = (s1_ref[h] - off[h:h + 1, :]) * LOG2E
        s2l_ref[h] = s2_ref[h] * LOG2E


def _peer_scores(qp, sk):
    T = qp.shape[0]
    tm = PEER_TM
    H = PEER_HEADS
    big = pl.BlockSpec((H, PEER_NKEYS, tm), lambda i: (0, 0, i))
    big_sds = jax.ShapeDtypeStruct((H, PEER_NKEYS, T), F32)
    return pl.pallas_call(
        _peer_scores_body,
        grid=(T // tm,),
        in_specs=[pl.BlockSpec((tm, qp.shape[1]), lambda i: (i, 0)),
                  pl.BlockSpec(sk.shape, lambda i: (0, 0, 0))],
        out_specs=[big, big, big, big, pl.BlockSpec((H, tm), lambda i: (0, i))],
        out_shape=[big_sds] * 4 + [jax.ShapeDtypeStruct((H, T), F32)],
        scratch_shapes=[pltpu.VMEM((2, PEER_TOPK, H, tm), F32)],
        compiler_params=_params("parallel"),
        name="peer_scores",
    )(qp, sk)


def _gelu_tanh(x):
    return 0.5 * x * (1.0 + jnp.tanh(math.sqrt(2.0 / math.pi) * (x + 0.044715 * (x * x * x))))


def _peer_main_body(x_ref, u_ref, vt_ref, s1_ref, s1l_ref, s2_ref, s2l_ref, tau_ref, o_ref, at_ref):
    e = pl.program_id(1)
    rows_per_s1_block = s1_ref.shape[1]
    i1_per_step = PEER_EB // PEER_NKEYS
    roff = (e % (rows_per_s1_block // i1_per_step)) * i1_per_step

    @pl.when(e == 0)
    def _():
        o_ref[...] = jnp.zeros_like(o_ref)

    x = x_ref[...]
    for sub in range(PEER_EB // PEER_SUB):
        ht = lax.dot_general(u_ref[sub * PEER_SUB:(sub + 1) * PEER_SUB, :], x,
                             (((1,), (1,)), ((), ())), preferred_element_type=F32)
        act = _gelu_tanh(ht)
        for ii in range(PEER_SUB // PEER_NKEYS):
            r = roff + sub * (PEER_SUB // PEER_NKEYS) + ii
            g = jnp.zeros((PEER_NKEYS, x.shape[0]), F32)
            for h in range(PEER_HEADS):
                s1 = s1_ref[h, pl.ds(r, 1), :]
                s1l = s1l_ref[h, pl.ds(r, 1), :]
                sel = (s1 + s2_ref[h]) >= tau_ref[h:h + 1, :]
                g = g + jnp.where(sel, jnp.exp2(s1l + s2l_ref[h]), 0.0)
            lo = sub * PEER_SUB + ii * PEER_NKEYS
            at_ref[lo:lo + PEER_NKEYS, :] = (g * act[ii * PEER_NKEYS:(ii + 1) * PEER_NKEYS, :]).astype(BF16)
    o_ref[...] += jnp.dot(vt_ref[...], at_ref[...], preferred_element_type=F32)


def _peer_main(xb, u, vt, s1, s1l, s2, s2l, tau):
    T, D = xb.shape
    tm, eb = PEER_TM, PEER_EB
    H = PEER_HEADS
    s1_rows = 8
    per = s1_rows // (eb // PEER_NKEYS)
    s1_spec = pl.BlockSpec((H, s1_rows, tm), lambda i, e: (0, e // per, i))
    s2_spec = pl.BlockSpec((H, PEER_NKEYS, tm), lambda i, e: (0, 0, i))
    return pl.pallas_call(
        _peer_main_body,
        grid=(T // tm, PEER_EXPERTS // eb),
        in_specs=[pl.BlockSpec((tm, D), lambda i, e: (i, 0)),
                  pl.BlockSpec((eb, D), lambda i, e: (e, 0)),
                  pl.BlockSpec((D, eb), lambda i, e: (0, e)),
                  s1_spec, s1_spec, s2_spec, s2_spec,
                  pl.BlockSpec((H, tm), lambda i, e: (0, i))],
        out_specs=pl.BlockSpec((D, tm), lambda i, e: (0, i)),
        out_shape=jax.ShapeDtypeStruct((D, T), F32),
        scratch_shapes=[pltpu.VMEM((eb, tm), BF16)],
        compiler_params=_params("parallel", "arbitrary"),
        name="peer_main",
    )(xb, u, vt, s1, s1l, s2, s2l, tau)


def _peer_layer(xf, xb, w_q, sub_keys, u_tab, v_tab, g, b):
    T, D = xf.shape
    qp = _matmul(xb, w_q.astype(BF16), tm=1024, tn=512, n_cols=w_q.shape[1], w_col_off=0,
                 epilogue=_epi_plain, extras=[],
                 outs=[(jax.ShapeDtypeStruct((T, w_q.shape[1]), BF16),
                        pl.BlockSpec((1024, 512), lambda i, j: (i, j)))],
                 name="peer_query")[0]
    sk = sub_keys.astype(BF16).reshape(PEER_HEADS * 2, PEER_NKEYS, LANES)
    s1, s1l, s2, s2l, tau = _peer_scores(qp, sk)
    ft = _peer_main(xb, u_tab.astype(BF16), v_tab.T.astype(BF16), s1, s1l, s2, s2l, tau)
    return _res_ln_t(ft, xf, g, b)


def _retention_layer(xf, xb, w_in, w_out, tabs, g, b, batch, seq):
    T = xf.shape[0]
    cos_r, sin_r = tabs[0], tabs[1]
    w_in = w_in.astype(BF16)
    tm, tn = 1024, 512
    tab_spec = pl.BlockSpec((tm, LANES), lambda i, j: (i, 0))
    out_spec = pl.BlockSpec((tm, tn), lambda i, j: (i, j))
    qk = _matmul(xb, w_in, tm=tm, tn=tn, n_cols=2 * RET_QK, w_col_off=0, epilogue=_epi_ret_rope,
                 extras=[(cos_r, tab_spec), (sin_r, tab_spec)],
                 outs=[(jax.ShapeDtypeStruct((T, 2 * RET_QK), BF16), out_spec)],
                 name="ret_qk_proj")[0]
    vg = _matmul(xb, w_in, tm=tm, tn=tn, n_cols=2 * RET_V, w_col_off=2 * RET_QK, epilogue=_epi_plain,
                 extras=[], outs=[(jax.ShapeDtypeStruct((T, 2 * RET_V), BF16), out_spec)],
                 name="ret_vg_proj")[0]
    h_idx = jnp.arange(RET_HEADS, dtype=F32)
    log_gamma = jnp.log(1.0 - jnp.exp2(-5.0 - h_idx))
    y = _retention_core(qk, vg, log_gamma, batch, seq)
    return _mm_res_ln(y, w_out.astype(BF16), xf, g, b, name="ret_out_ln")


def _mla_shared(xb, w_dkv, kv_norm, w_ukv, tabs):
    T = xb.shape[0]
    cm, s1m, s2m = tabs[2], tabs[3], tabs[4]
    tm = 1024
    n_pad = MLA_KV_RANK + LANES
    w = jnp.pad(w_dkv, ((0, 0), (0, n_pad - w_dkv.shape[1]))).astype(BF16)
    tab_spec = pl.BlockSpec((tm, LANES), lambda i, j: (i, 0))
    c, kr = _matmul(
        xb, w, tm=tm, tn=n_pad, n_cols=n_pad, w_col_off=0, epilogue=_epi_dkv,
        extras=[(kv_norm.reshape(1, -1), pl.BlockSpec((1, MLA_KV_RANK), lambda i, j: (0, 0))),
                (cm, tab_spec), (s1m, tab_spec), (s2m, tab_spec)],
        outs=[(jax.ShapeDtypeStruct((T, MLA_KV_RANK), BF16), pl.BlockSpec((tm, MLA_KV_RANK), lambda i, j: (i, 0))),
              (jax.ShapeDtypeStruct((T, LANES), BF16), pl.BlockSpec((tm, LANES), lambda i, j: (i, 0)))],
        name="mla_dkv")
    w3 = w_ukv.reshape(MLA_KV_RANK, MLA_HEADS, MLA_NOPE + MLA_V)
    wk = w3[:, :, :MLA_NOPE].reshape(MLA_KV_RANK, MLA_HEADS * MLA_NOPE).astype(BF16)
    wv = w3[:, :, MLA_NOPE:].reshape(MLA_KV_RANK, MLA_HEADS * MLA_V).astype(BF16)
    tn = 2 * MLA_NOPE
    kf = _matmul(c, wk, tm=tm, tn=tn, n_cols=wk.shape[1], w_col_off=0, epilogue=_epi_mla_k,
                 extras=[(kr, tab_spec)],
                 outs=[(jax.ShapeDtypeStruct((T, MLA_HEADS * MLA_QK_PAD), BF16),
                        pl.BlockSpec((tm, 2 * MLA_QK_PAD), lambda i, j: (i, j)))],
                 name="mla_k_up")[0]
    v = _matmul(c, wv, tm=tm, tn=512, n_cols=wv.shape[1], w_col_off=0, epilogue=_epi_plain, extras=[],
                outs=[(jax.ShapeDtypeStruct((T, MLA_HEADS * MLA_V), BF16),
                       pl.BlockSpec((tm, 512), lambda i, j: (i, j)))],
                name="mla_v_up")[0]
    return kf, v


def _mla_layer(xf, xb, shared, w_dq, q_norm, w_uq, w_o, tabs, g, b, batch, seq):
    T = xf.shape[0]
    kf, v = shared
    cm, s1m, s2m = tabs[2], tabs[3], tabs[4]
    tm = 1024
    cq = _matmul(xb, w_dq.astype(BF16), tm=tm, tn=MLA_Q_RANK, n_cols=MLA_Q_RANK, w_col_off=0,
                 epilogue=_epi_rms,
                 extras=[(q_norm.reshape(1, -1), pl.BlockSpec((1, MLA_Q_RANK), lambda i, j: (0, 0)))],
                 outs=[(jax.ShapeDtypeStruct((T, MLA_Q_RANK), BF16),
                        pl.BlockSpec((tm, MLA_Q_RANK), lambda i, j: (i, 0)))],
                 name="mla_dq")[0]
    w3 = w_uq.reshape(MLA_Q_RANK, MLA_HEADS, MLA_NOPE + MLA_ROPE)
    w3 = jnp.pad(w3, ((0, 0), (0, 0), (0, MLA_QK_PAD - MLA_NOPE - MLA_ROPE)))
    wq = w3.reshape(MLA_Q_RANK, MLA_HEADS * MLA_QK_PAD).astype(BF16)
    tab_spec = pl.BlockSpec((tm, LANES), lambda i, j: (i, 0))
    qf = _matmul(cq, wq, tm=tm, tn=512, n_cols=wq.shape[1], w_col_off=0, epilogue=_epi_mla_q,
                 extras=[(cm, tab_spec), (s1m, tab_spec), (s2m, tab_spec)],
                 outs=[(jax.ShapeDtypeStruct((T, MLA_HEADS * MLA_QK_PAD), BF16),
                        pl.BlockSpec((tm, 512), lambda i, j: (i, j)))],
                 name="mla_q_up")[0]
    o = _mla_attention(qf, kf, v, batch, seq)
    return _mm_res_ln(o, w_o.astype(BF16), xf, g, b, name="mla_out_ln")


def kernel(x, positions, ret_w_in, ret_w_out, mla_w_dkv, mla_kv_norm, mla_w_ukv, mla_w_dq, mla_q_norm,
           mla_w_uq, mla_w_o, peer_w_q, peer_sub_keys, peer_u, peer_v, ln_g, ln_b):
    B, S, D = x.shape
    T = B * S
    xf = x.reshape(T, D)
    xb = xf.astype(BF16)
    pos_b = jnp.broadcast_to(positions.reshape(T, 1).astype(F32), (T, LANES))
    tabs = _rope_tables(pos_b)
    shared = None
    for l in range(DEPTH):
        g0, b0 = ln_g[l, 0].reshape(1, D), ln_b[l, 0].reshape(1, D)
        g1, b1 = ln_g[l, 1].reshape(1, D), ln_b[l, 1].reshape(1, D)
        if l < N_A:
            xf, xb = _retention_layer(xf, xb, ret_w_in[l], ret_w_out[l], tabs, g0, b0, B, S)
        else:
            if shared is None:
                shared = _mla_shared(xb, mla_w_dkv, mla_kv_norm, mla_w_ukv, tabs)
            j = l - N_A
            xf, xb = _mla_layer(xf, xb, shared, mla_w_dq[j], mla_q_norm[j], mla_w_uq[j], mla_w_o[j],
                                tabs, g0, b0, B, S)
        xf, xb = _peer_layer(xf, xb, peer_w_q[l], peer_sub_keys[l], peer_u[l], peer_v[l], g1, b1)
    return xf.reshape(B, S, D)
```

```python
import functools
import math

import jax
import jax.numpy as jnp
from jax import lax
from jax.experimental import pallas as pl
from jax.experimental.pallas import tpu as pltpu

F32 = jnp.float32
BF16 = jnp.bfloat16

D_MODEL = 2048
DEPTH = 4
CHUNK = 64
ROPE_BASE = 10000.0
N_A = DEPTH // 2
ALPHA = (2 * DEPTH) ** 0.25
NORM_EPS = 1e-5

RET_HEADS = 8
RET_QK_DIM = 256
RET_V_DIM = 512
RET_QK = RET_HEADS * RET_QK_DIM
RET_V = RET_HEADS * RET_V_DIM
RET_SUPER = 256

MLA_HEADS = 16
MLA_NOPE = 128
MLA_ROPE = 64
MLA_V = 128
MLA_KV_RANK = 512
MLA_Q_RANK = 512
MLA_QK_PAD = 256
MLA_TQ = 512
MLA_TK = 512
MLA_HG = 2

PEER_HEADS = 8
PEER_NKEYS = 128
PEER_EXPERTS = PEER_NKEYS * PEER_NKEYS
PEER_TOPK = 16
PEER_TM = 1024
PEER_EB = 512
PEER_TC = 128
PEER_TR = 128
PEER_TN = 512
PEER_MU = 256
PEER_SCORE_TM = 512

LANES = 128
VMEM_LIMIT = 56 * 1024 * 1024
LOG2E = 1.4426950408889634


def _chunk_of(idx):
    return lax.shift_right_logical(idx, int(math.log2(CHUNK)))


def _params(*sem, flags=None):
    return pltpu.CompilerParams(dimension_semantics=sem, vmem_limit_bytes=VMEM_LIMIT, flags=flags)


def _rope_tables_body(pos_ref, fr_ref, fm_ref, cr_ref, sr_ref, cm_ref, s1_ref, s2_ref):
    pos = pos_ref[...]
    ang = pos * fr_ref[...]
    cr_ref[...] = jnp.cos(ang)
    sr_ref[...] = jnp.sin(ang)
    angm = pos * fm_ref[...]
    lane = lax.broadcasted_iota(jnp.int32, angm.shape, 1)
    half = MLA_ROPE // 2
    c = jnp.cos(angm)
    s = jnp.sin(angm)
    cm_ref[...] = jnp.where(lane < MLA_ROPE, c, 0.0)
    s1_ref[...] = jnp.where(lane < half, -s, 0.0)
    s2_ref[...] = jnp.where((lane >= half) & (lane < MLA_ROPE), s, 0.0)


def _rope_tables(pos_b):
    T = pos_b.shape[0]
    tm = 1024
    half_r = RET_QK_DIM // 2
    fr = ROPE_BASE ** (-jnp.arange(half_r, dtype=F32) / half_r)
    half_m = MLA_ROPE // 2
    fm = ROPE_BASE ** (-jnp.arange(half_m, dtype=F32) / half_m)
    fm = jnp.concatenate([fm, fm, jnp.zeros((LANES - MLA_ROPE,), F32)])
    row = pl.BlockSpec((tm, LANES), lambda i: (i, 0))
    one = pl.BlockSpec((1, LANES), lambda i: (0, 0))
    sds = jax.ShapeDtypeStruct((T, LANES), F32)
    return pl.pallas_call(
        _rope_tables_body,
        grid=(T // tm,),
        in_specs=[row, one, one],
        out_specs=[row] * 5,
        out_shape=[sds] * 5,
        compiler_params=_params("arbitrary"),
        name="rope_tables",
    )(pos_b, fr.reshape(1, LANES), fm.reshape(1, LANES))


def _rope_small(r, c, s1, s2):
    return r * c + pltpu.roll(r, 96, 1) * s1 + pltpu.roll(r, 32, 1) * s2


def _mm_body(*refs, n_extra, epilogue):
    a_ref, w_ref = refs[:2]
    extras = refs[2:2 + n_extra]
    outs = refs[2 + n_extra:]
    acc = jnp.dot(a_ref[...], w_ref[...], preferred_element_type=F32)
    epilogue(acc, extras, outs)


def _matmul(a, w, *, tm, tn, n_cols, w_col_off, epilogue, extras, outs, name):
    M, K = a.shape
    off = w_col_off // tn
    in_specs = [pl.BlockSpec((tm, K), lambda i, j: (i, 0)),
                pl.BlockSpec((K, tn), lambda i, j: (0, j + off))]
    in_specs += [s for _, s in extras]
    return pl.pallas_call(
        functools.partial(_mm_body, n_extra=len(extras), epilogue=epilogue),
        grid=(M // tm, n_cols // tn),
        in_specs=in_specs,
        out_specs=[s for _, s in outs],
        out_shape=[s for s, _ in outs],
        compiler_params=_params("parallel", "arbitrary"),
        name=name,
    )(a, w, *[x for x, _ in extras])


def _epi_plain(acc, extras, outs):
    outs[0][...] = acc.astype(outs[0].dtype)


def _epi_ret_rope(acc, extras, outs):
    cos = extras[0][...]
    sin = extras[1][...]
    j = pl.program_id(1)
    scale = jnp.where(j * acc.shape[1] >= RET_QK, RET_QK_DIM ** -0.5, 1.0).astype(F32)
    half = RET_QK_DIM // 2
    for hh in range(acc.shape[1] // RET_QK_DIM):
        s0 = hh * RET_QK_DIM
        x1 = acc[:, s0:s0 + half]
        x2 = acc[:, s0 + half:s0 + RET_QK_DIM]
        outs[0][:, s0:s0 + half] = ((x1 * cos - x2 * sin) * scale).astype(outs[0].dtype)
        outs[0][:, s0 + half:s0 + RET_QK_DIM] = ((x2 * cos + x1 * sin) * scale).astype(outs[0].dtype)


def _rms(acc, g):
    ms = jnp.mean(acc * acc, axis=-1, keepdims=True)
    return acc * lax.rsqrt(ms + NORM_EPS) * g


def _epi_rms(acc, extras, outs):
    outs[0][...] = _rms(acc, extras[0][...]).astype(outs[0].dtype)


def _epi_dkv(acc, extras, outs):
    g, c, s1, s2 = (e[...] for e in extras)
    outs[0][...] = _rms(acc[:, :MLA_KV_RANK], g).astype(outs[0].dtype)
    r = acc[:, MLA_KV_RANK:MLA_KV_RANK + LANES]
    outs[1][...] = _rope_small(r, c, s1, s2).astype(outs[1].dtype)


def _epi_mla_q(acc, extras, outs):
    c, s1, s2 = (e[...] for e in extras)
    for hh in range(acc.shape[1] // MLA_QK_PAD):
        s0 = hh * MLA_QK_PAD
        outs[0][:, s0:s0 + MLA_NOPE] = acc[:, s0:s0 + MLA_NOPE].astype(outs[0].dtype)
        r = acc[:, s0 + MLA_NOPE:s0 + MLA_QK_PAD]
        outs[0][:, s0 + MLA_NOPE:s0 + MLA_QK_PAD] = _rope_small(r, c, s1, s2).astype(outs[0].dtype)


def _epi_mla_k(acc, extras, outs):
    kr = extras[0][...]
    for hh in range(acc.shape[1] // MLA_NOPE):
        s0 = hh * MLA_QK_PAD
        outs[0][:, s0:s0 + MLA_NOPE] = acc[:, hh * MLA_NOPE:(hh + 1) * MLA_NOPE].astype(outs[0].dtype)
        outs[0][:, s0 + MLA_NOPE:s0 + MLA_QK_PAD] = kr


def _ln(z, g, b):
    mu = jnp.mean(z, axis=-1, keepdims=True)
    zc = z - mu
    var = jnp.mean(zc * zc, axis=-1, keepdims=True)
    return zc * lax.rsqrt(var + NORM_EPS) * g + b


def _mm_res_ln_body(a_ref, w_ref, x_ref, g_ref, b_ref, of_ref, ot_ref, acc_ref, *, nk):
    k = pl.program_id(1)

    @pl.when(k == 0)
    def _():
        acc_ref[...] = jnp.zeros_like(acc_ref)

    acc_ref[...] += jnp.dot(a_ref[...], w_ref[...], preferred_element_type=F32)

    @pl.when(k == nk - 1)
    def _():
        y = _ln(ALPHA * x_ref[...] + acc_ref[...], g_ref[...], b_ref[...])
        of_ref[...] = y
        ot_ref[...] = y.T.astype(BF16)


def _mm_res_ln(a, w, x, g, b, *, tm=512, tk=1024, name):
    M, K = a.shape
    N = w.shape[1]
    tk = min(tk, K)
    nk = K // tk
    row = pl.BlockSpec((tm, N), lambda i, k: (i, 0))
    vec = pl.BlockSpec((1, N), lambda i, k: (0, 0))
    return pl.pallas_call(
        functools.partial(_mm_res_ln_body, nk=nk),
        grid=(M // tm, nk),
        in_specs=[pl.BlockSpec((tm, tk), lambda i, k: (i, k)),
                  pl.BlockSpec((tk, N), lambda i, k: (k, 0)),
                  row, vec, vec],
        out_specs=[row, pl.BlockSpec((N, tm), lambda i, k: (0, i))],
        out_shape=[jax.ShapeDtypeStruct((M, N), F32), jax.ShapeDtypeStruct((N, M), BF16)],
        scratch_shapes=[pltpu.VMEM((tm, N), F32)],
        compiler_params=_params("parallel", "arbitrary"),
        name=name,
    )(a, w, x, g, b)


def _res_ln_t_body(ft_ref, x_ref, g_ref, b_ref, of_ref, ob_ref):
    y = _ln(ALPHA * x_ref[...] + ft_ref[...].T, g_ref[...], b_ref[...])
    of_ref[...] = y
    ob_ref[...] = y.astype(BF16)


def _res_ln_t(ft, x, g, b, *, tm=512):
    M, N = x.shape
    row = pl.BlockSpec((tm, N), lambda i: (i, 0))
    vec = pl.BlockSpec((1, N), lambda i: (0, 0))
    return pl.pallas_call(
        _res_ln_t_body,
        grid=(M // tm,),
        in_specs=[pl.BlockSpec((N, tm), lambda i: (0, i)), row, vec, vec],
        out_specs=[row, row],
        out_shape=[jax.ShapeDtypeStruct((M, N), F32), jax.ShapeDtypeStruct((M, N), BF16)],
        compiler_params=_params("parallel"),
        name="peer_res_ln",
    )(ft, x, g, b)


def _ret_body(lg_ref, q_ref, k_ref, v_ref, gate_ref, o_ref, state_ref):
    h = pl.program_id(1)
    c = pl.program_id(2)
    sc = RET_SUPER

    @pl.when(c == 0)
    def _():
        state_ref[...] = jnp.zeros_like(state_ref)

    lg = lg_ref[h]
    n = lax.broadcasted_iota(jnp.int32, (sc, sc), 0)
    m = lax.broadcasted_iota(jnp.int32, (sc, sc), 1)
    cn = _chunk_of(n)
    cm = _chunk_of(m)
    dist = jnp.where(cn == cm, jnp.abs(n - m), n - m).astype(F32)
    dmat = jnp.where(cm <= cn, jnp.exp(lg * dist), 0.0)
    ncol = lax.broadcasted_iota(jnp.int32, (sc, 1), 0).astype(F32)
    q_dec = jnp.exp(lg * (ncol + 1.0))
    k_dec = jnp.exp(lg * (sc - 1.0 - ncol))
    c_dec = jnp.exp(jnp.full((1, RET_V_DIM), lg * sc, F32))

    q = q_ref[...]
    k = k_ref[...]
    v = v_ref[...]
    state = state_ref[...]
    scores = lax.dot_general(q, k, (((1,), (1,)), ((), ())), preferred_element_type=F32) * dmat
    inner = jnp.dot(scores.astype(BF16), v, preferred_element_type=F32)
    cross = jnp.dot(q, state.astype(BF16), preferred_element_type=F32) * q_dec
    kd = (k.astype(F32) * k_dec).astype(BF16)
    state_ref[...] = state * c_dec + lax.dot_general(
        kd, v, (((0,), (0,)), ((), ())), preferred_element_type=F32)

    o = inner + cross
    mu = jnp.mean(o, axis=-1, keepdims=True)
    oc = o - mu
    var = jnp.mean(oc * oc, axis=-1, keepdims=True)
    on = oc * lax.rsqrt(var + NORM_EPS)
    g = gate_ref[...].astype(F32)
    o_ref[...] = (g * jax.nn.sigmoid(g) * on).astype(o_ref.dtype)


def _retention_core(qk, vg, log_gamma, batch, seq):
    T = qk.shape[0]
    sc = RET_SUPER
    nsc = seq // sc
    H = RET_HEADS
    row = lambda b, h, c: b * nsc + c
    return pl.pallas_call(
        _ret_body,
        grid=(batch, H, nsc),
        in_specs=[pl.BlockSpec(memory_space=pltpu.SMEM),
                  pl.BlockSpec((sc, RET_QK_DIM), lambda b, h, c: (row(b, h, c), h)),
                  pl.BlockSpec((sc, RET_QK_DIM), lambda b, h, c: (row(b, h, c), H + h)),
                  pl.BlockSpec((sc, RET_V_DIM), lambda b, h, c: (row(b, h, c), h)),
                  pl.BlockSpec((sc, RET_V_DIM), lambda b, h, c: (row(b, h, c), H + h))],
        out_specs=pl.BlockSpec((sc, RET_V_DIM), lambda b, h, c: (row(b, h, c), h)),
        out_shape=jax.ShapeDtypeStruct((T, RET_V), BF16),
        scratch_shapes=[pltpu.VMEM((RET_QK_DIM, RET_V_DIM), F32)],
        compiler_params=_params("parallel", "parallel", "arbitrary"),
        name="retention_core",
    )(log_gamma, qk, qk, vg, vg)


def _mla_attn_body(q_ref, k_ref, v_ref, o_ref, *, scale):
    i = pl.program_id(2)
    tq, tk = MLA_TQ, MLA_TK
    c = scale * LOG2E
    heads = range(MLA_HG)

    def block(j, carry, masked):
        rows = pl.ds(pl.multiple_of(j * tk, tk), tk)
        out = []
        for hh in heads:
            m_prev, l_prev, acc = carry[hh]
            q = q_ref[:, hh * MLA_QK_PAD:(hh + 1) * MLA_QK_PAD]
            k = k_ref[rows, hh * MLA_QK_PAD:(hh + 1) * MLA_QK_PAD]
            v = v_ref[rows, hh * MLA_V:(hh + 1) * MLA_V]
            s = lax.dot_general(q, k, (((1,), (1,)), ((), ())), preferred_element_type=F32)
            if masked:
                qc = _chunk_of(lax.broadcasted_iota(jnp.int32, (tq, tk), 0))
                kc = _chunk_of(lax.broadcasted_iota(jnp.int32, (tq, tk), 1))
                s = jnp.where(kc <= qc, s, -jnp.inf)
            m_new = jnp.maximum(m_prev, jnp.max(s, axis=-1, keepdims=True))
            a = jnp.exp2((m_prev - m_new) * c)
            p = jnp.exp2((s - m_new) * c)
            l_new = a * l_prev + jnp.sum(p, axis=-1, keepdims=True)
            acc = a * acc + jnp.dot(p.astype(BF16), v, preferred_element_type=F32)
            out.append((m_new, l_new, acc))
        return tuple(out)

    init = tuple((jnp.full((tq, 1), -jnp.inf, F32), jnp.zeros((tq, 1), F32),
                  jnp.zeros((tq, MLA_V), F32)) for _ in heads)
    carry = lax.fori_loop(0, i, lambda j, cr: block(j, cr, False), init)
    final = block(i, carry, True)
    for hh in heads:
        m, l, acc = final[hh]
        o_ref[:, hh * MLA_V:(hh + 1) * MLA_V] = (acc / l).astype(o_ref.dtype)


def _mla_attention(qf, kf, v, batch, seq):
    T = qf.shape[0]
    assert MLA_TQ == MLA_TK and MLA_TQ % CHUNK == 0
    nq = seq // MLA_TQ
    hg = MLA_HG
    scale = (MLA_NOPE + MLA_ROPE) ** -0.5
    return pl.pallas_call(
        functools.partial(_mla_attn_body, scale=scale),
        grid=(batch, MLA_HEADS // hg, nq),
        in_specs=[pl.BlockSpec((MLA_TQ, hg * MLA_QK_PAD), lambda b, h, i: (b * nq + i, h)),
                  pl.BlockSpec((seq, hg * MLA_QK_PAD), lambda b, h, i: (b, h)),
                  pl.BlockSpec((seq, hg * MLA_V), lambda b, h, i: (b, h))],
        out_specs=pl.BlockSpec((MLA_TQ, hg * MLA_V), lambda b, h, i: (b * nq + i, h)),
        out_shape=jax.ShapeDtypeStruct((T, MLA_HEADS * MLA_V), BF16),
        compiler_params=_params("parallel", "parallel", "arbitrary"),
        name="mla_attention",
    )(qf, kf, v)


_PEER_PAIRS = [(i, j) for i in range(PEER_TOPK) for j in range(PEER_TOPK)
               if (i + 1) * (j + 1) <= PEER_TOPK]


def _peer_scores_body(q_ref, sk_ref, s1p_ref, s2l_ref, thr_ref, s1_scr, top_ref):
    H = PEER_HEADS
    neg = -jnp.inf
    for h in range(H):
        for p in range(2):
            q = q_ref[(2 * h + p) * LANES:(2 * h + p + 1) * LANES, :]
            st = jnp.dot(sk_ref[2 * h + p], q, preferred_element_type=F32)
            if p == 0:
                s1_scr[h] = st
            else:
                s2l_ref[h] = st * LOG2E
            vals = st
            for r in range(PEER_TOPK):
                mx = jnp.max(vals, axis=0, keepdims=True)
                top_ref[p, r, h:h + 1, :] = mx
                vals = jnp.where(vals == mx, neg, vals)

    cands = [top_ref[0, i] + top_ref[1, j] for i, j in _PEER_PAIRS]
    tau = jnp.full_like(cands[0], neg)
    for ci in cands:
        cnt = jnp.zeros_like(ci)
        for cj in cands:
            cnt = cnt + jnp.where(cj >= ci, 1.0, 0.0)
        tau = jnp.maximum(tau, jnp.where(cnt >= float(PEER_TOPK), ci, neg))
    mx = cands[0]
    z = jnp.zeros_like(mx)
    for ci in cands:
        z = z + jnp.where(ci >= tau, jnp.exp(ci - mx), 0.0)
    off_l = (mx + jnp.log(z)) * LOG2E
    thr = jnp.full_like(mx, jnp.inf)
    for (i, j), ci in zip(_PEER_PAIRS, cands):
        cp = (top_ref[0, i] * LOG2E - off_l) + top_ref[1, j] * LOG2E
        thr = jnp.minimum(thr, jnp.where(ci >= tau, cp, jnp.inf))
    thr_ref[...] = thr
    for h in range(H):
        s1p_ref[h] = s1_scr[h] * LOG2E - off_l[h:h + 1, :]


def _peer_scores(qt, sk):
    T = qt.shape[1]
    tm = PEER_SCORE_TM
    H = PEER_HEADS
    big = pl.BlockSpec((H, PEER_NKEYS, tm), lambda i: (0, 0, i))
    big_sds = jax.ShapeDtypeStruct((H, PEER_NKEYS, T), F32)
    return pl.pallas_call(
        _peer_scores_body,
        grid=(T // tm,),
        in_specs=[pl.BlockSpec((qt.shape[0], tm), lambda i: (0, i)),
                  pl.BlockSpec(sk.shape, lambda i: (0, 0, 0))],
        out_specs=[big, big, pl.BlockSpec((H, tm), lambda i: (0, i))],
        out_shape=[big_sds, big_sds, jax.ShapeDtypeStruct((H, T), F32)],
        scratch_shapes=[pltpu.VMEM((H, PEER_NKEYS, tm), F32),
                        pltpu.VMEM((2, PEER_TOPK, H, tm), F32)],
        compiler_params=_params("parallel"),
        name="peer_scores",
    )(qt, sk)


_GELU_K0 = math.sqrt(2.0 / math.pi)
_GELU_K1 = _GELU_K0 * 0.044715


def _peer_main_body(xt_ref, u_ref, vt_ref, s1p_ref, s2l_ref, thr_ref, o_ref, ata_ref, atb_ref, ht_ref,
                    *, n_blocks):
    e = pl.program_id(1)
    tm = xt_ref.shape[1]
    D = o_ref.shape[0]
    i1_per_step = PEER_EB // PEER_NKEYS
    e_c = jnp.minimum(e, n_blocks - 1)
    roff = (e_c % (s1p_ref.shape[1] // i1_per_step)) * i1_per_step

    @pl.when(e == 0)
    def _():
        o_ref[...] = jnp.zeros_like(o_ref)
        atb_ref[...] = jnp.zeros_like(atb_ref)

    tn, mu, tc = PEER_TN, PEER_MU, PEER_TC
    n_mu = PEER_EB // mu
    mv = D // n_mu

    def step(cur_ref, prev_ref):
        for th in range(tm // tn):
            tcols = slice(th * tn, (th + 1) * tn)
            for mh in range(n_mu):
                urows = slice(mh * mu, (mh + 1) * mu)
                ht_ref[urows, tcols] = jnp.dot(u_ref[urows, :], xt_ref[:, tcols],
                                               preferred_element_type=F32)
                vrows = slice(mh * mv, (mh + 1) * mv)
                o_ref[vrows, tcols] += jnp.dot(vt_ref[vrows, :], prev_ref[:, tcols],
                                               preferred_element_type=F32)
                for ii in range(mu // PEER_NKEYS):
                    i1 = mh * (mu // PEER_NKEYS) + ii
                    s1rows = [s1p_ref[h, pl.ds(roff + i1, 1), :] for h in range(PEER_HEADS)]
                    for c in range(tn // tc):
                        cols = slice(th * tn + c * tc, th * tn + (c + 1) * tc)
                        for kk in range(PEER_NKEYS // PEER_TR):
                            krows = slice(kk * PEER_TR, (kk + 1) * PEER_TR)
                            rows = slice(i1 * PEER_NKEYS + kk * PEER_TR, i1 * PEER_NKEYS + (kk + 1) * PEER_TR)
                            g = jnp.zeros((PEER_TR, tc), F32)
                            for h in range(PEER_HEADS):
                                v = s1rows[h][:, cols] + s2l_ref[h, krows, cols]
                                g = g + jnp.where(v >= thr_ref[h:h + 1, cols], jnp.exp2(v), 0.0)
                            hc = ht_ref[rows, cols]
                            tnh = jnp.tanh(hc * (_GELU_K0 + _GELU_K1 * (hc * hc)))
                            hx = 0.5 * hc
                            cur_ref[rows, cols] = (g * (hx + hx * tnh)).astype(BF16)

    @pl.when(e % 2 == 0)
    def _():
        step(ata_ref, atb_ref)

    @pl.when(e % 2 == 1)
    def _():
        step(atb_ref, ata_ref)


def _peer_main(xt, u, vt, s1p, s2l, thr):
    D, T = xt.shape
    tm, eb = PEER_TM, PEER_EB
    H = PEER_HEADS
    nb = PEER_EXPERTS // eb
    s1_rows = 8
    per = s1_rows // (eb // PEER_NKEYS)
    cur = lambda e: jnp.minimum(e, nb - 1)
    prev = lambda e: jnp.maximum(e - 1, 0)
    return pl.pallas_call(
        functools.partial(_peer_main_body, n_blocks=nb),
        grid=(T // tm, nb + 1),
        in_specs=[pl.BlockSpec((D, tm), lambda i, e: (0, i)),
                  pl.BlockSpec((eb, D), lambda i, e: (cur(e), 0)),
                  pl.BlockSpec((D, eb), lambda i, e: (0, prev(e))),
                  pl.BlockSpec((H, s1_rows, tm), lambda i, e: (0, cur(e) // per, i)),
                  pl.BlockSpec((H, PEER_NKEYS, tm), lambda i, e: (0, 0, i)),
                  pl.BlockSpec((H, tm), lambda i, e: (0, i))],
        out_specs=pl.BlockSpec((D, tm), lambda i, e: (0, i)),
        out_shape=jax.ShapeDtypeStruct((D, T), F32),
        scratch_shapes=[pltpu.VMEM((eb, tm), BF16), pltpu.VMEM((eb, tm), BF16),
                        pltpu.VMEM((eb, tm), F32)],
        compiler_params=_params("parallel", "arbitrary"),
        name="peer_main",
    )(xt, u, vt, s1p, s2l, thr)


def _peer_layer(xf, xt, w_q, sub_keys, u_tab, v_tab, g, b):
    D, T = xt.shape
    nq = w_q.shape[1]
    qt = _matmul(w_q.T.astype(BF16), xt, tm=1024, tn=1024, n_cols=T, w_col_off=0,
                 epilogue=_epi_plain, extras=[],
                 outs=[(jax.ShapeDtypeStruct((nq, T), BF16),
                        pl.BlockSpec((1024, 1024), lambda i, j: (i, j)))],
                 name="peer_query")[0]
    sk = sub_keys.astype(BF16).reshape(PEER_HEADS * 2, PEER_NKEYS, LANES)
    s1p, s2l, thr = _peer_scores(qt, sk)
    ft = _peer_main(xt, u_tab.astype(BF16), v_tab.T.astype(BF16), s1p, s2l, thr)
    return _res_ln_t(ft, xf, g, b)


def _retention_layer(xf, xb, w_in, w_out, tabs, g, b, batch, seq):
    T = xf.shape[0]
    cos_r, sin_r = tabs[0], tabs[1]
    w_in = w_in.astype(BF16)
    tm, tn = 1024, 512
    tab_spec = pl.BlockSpec((tm, LANES), lambda i, j: (i, 0))
    out_spec = pl.BlockSpec((tm, tn), lambda i, j: (i, j))
    qk = _matmul(xb, w_in, tm=tm, tn=tn, n_cols=2 * RET_QK, w_col_off=0, epilogue=_epi_ret_rope,
                 extras=[(cos_r, tab_spec), (sin_r, tab_spec)],
                 outs=[(jax.ShapeDtypeStruct((T, 2 * RET_QK), BF16), out_spec)],
                 name="ret_qk_proj")[0]
    vg = _matmul(xb, w_in, tm=tm, tn=tn, n_cols=2 * RET_V, w_col_off=2 * RET_QK, epilogue=_epi_plain,
                 extras=[], outs=[(jax.ShapeDtypeStruct((T, 2 * RET_V), BF16), out_spec)],
                 name="ret_vg_proj")[0]
    h_idx = jnp.arange(RET_HEADS, dtype=F32)
    log_gamma = jnp.log(1.0 - jnp.exp2(-5.0 - h_idx))
    y = _retention_core(qk, vg, log_gamma, batch, seq)
    return _mm_res_ln(y, w_out.astype(BF16), xf, g, b, name="ret_out_ln")


def _mla_shared(xb, w_dkv, kv_norm, w_ukv, tabs):
    T = xb.shape[0]
    cm, s1m, s2m = tabs[2], tabs[3], tabs[4]
    tm = 1024
    n_pad = MLA_KV_RANK + LANES
    w = jnp.pad(w_dkv, ((0, 0), (0, n_pad - w_dkv.shape[1]))).astype(BF16)
    tab_spec = pl.BlockSpec((tm, LANES), lambda i, j: (i, 0))
    c, kr = _matmul(
        xb, w, tm=tm, tn=n_pad, n_cols=n_pad, w_col_off=0, epilogue=_epi_dkv,
        extras=[(kv_norm.reshape(1, -1), pl.BlockSpec((1, MLA_KV_RANK), lambda i, j: (0, 0))),
                (cm, tab_spec), (s1m, tab_spec), (s2m, tab_spec)],
        outs=[(jax.ShapeDtypeStruct((T, MLA_KV_RANK), BF16), pl.BlockSpec((tm, MLA_KV_RANK), lambda i, j: (i, 0))),
              (jax.ShapeDtypeStruct((T, LANES), BF16), pl.BlockSpec((tm, LANES), lambda i, j: (i, 0)))],
        name="mla_dkv")
    w3 = w_ukv.reshape(MLA_KV_RANK, MLA_HEADS, MLA_NOPE + MLA_V)
    wk = w3[:, :, :MLA_NOPE].reshape(MLA_KV_RANK, MLA_HEADS * MLA_NOPE).astype(BF16)
    wv = w3[:, :, MLA_NOPE:].reshape(MLA_KV_RANK, MLA_HEADS * MLA_V).astype(BF16)
    tn = 2 * MLA_NOPE
    kf = _matmul(c, wk, tm=tm, tn=tn, n_cols=wk.shape[1], w_col_off=0, epilogue=_epi_mla_k,
                 extras=[(kr, tab_spec)],
                 outs=[(jax.ShapeDtypeStruct((T, MLA_HEADS * MLA_QK_PAD), BF16),
                        pl.BlockSpec((tm, 2 * MLA_QK_PAD), lambda i, j: (i, j)))],
                 name="mla_k_up")[0]
    v = _matmul(c, wv, tm=tm, tn=512, n_cols=wv.shape[1], w_col_off=0, epilogue=_epi_plain, extras=[],
                outs=[(jax.ShapeDtypeStruct((T, MLA_HEADS * MLA_V), BF16),
                       pl.BlockSpec((tm, 512), lambda i, j: (i, j)))],
                name="mla_v_up")[0]
    return kf, v


def _mla_layer(xf, xb, shared, w_dq, q_norm, w_uq, w_o, tabs, g, b, batch, seq):
    T = xf.shape[0]
    kf, v = shared
    cm, s1m, s2m = tabs[2], tabs[3], tabs[4]
    tm = 1024
    cq = _matmul(xb, w_dq.astype(BF16), tm=tm, tn=MLA_Q_RANK, n_cols=MLA_Q_RANK, w_col_off=0,
                 epilogue=_epi_rms,
                 extras=[(q_norm.reshape(1, -1), pl.BlockSpec((1, MLA_Q_RANK), lambda i, j: (0, 0)))],
                 outs=[(jax.ShapeDtypeStruct((T, MLA_Q_RANK), BF16),
                        pl.BlockSpec((tm, MLA_Q_RANK), lambda i, j: (i, 0)))],
                 name="mla_dq")[0]
    w3 = w_uq.reshape(MLA_Q_RANK, MLA_HEADS, MLA_NOPE + MLA_ROPE)
    w3 = jnp.pad(w3, ((0, 0), (0, 0), (0, MLA_QK_PAD - MLA_NOPE - MLA_ROPE)))
    wq = w3.reshape(MLA_Q_RANK, MLA_HEADS * MLA_QK_PAD).astype(BF16)
    tab_spec = pl.BlockSpec((tm, LANES), lambda i, j: (i, 0))
    qf = _matmul(cq, wq, tm=tm, tn=512, n_cols=wq.shape[1], w_col_off=0, epilogue=_epi_mla_q,
                 extras=[(cm, tab_spec), (s1m, tab_spec), (s2m, tab_spec)],
                 outs=[(jax.ShapeDtypeStruct((T, MLA_HEADS * MLA_QK_PAD), BF16),
                        pl.BlockSpec((tm, 512), lambda i, j: (i, j)))],
                 name="mla_q_up")[0]
    o = _mla_attention(qf, kf, v, batch, seq)
    return _mm_res_ln(o, w_o.astype(BF16), xf, g, b, name="mla_out_ln")


def kernel(x, positions, ret_w_in, ret_w_out, mla_w_dkv, mla_kv_norm, mla_w_ukv, mla_w_dq, mla_q_norm,
           mla_w_uq, mla_w_o, peer_w_q, peer_sub_keys, peer_u, peer_v, ln_g, ln_b):
    B, S, D = x.shape
    T = B * S
    xf = x.reshape(T, D)
    xb = xf.astype(BF16)
    pos_b = jnp.broadcast_to(positions.reshape(T, 1).astype(F32), (T, LANES))
    tabs = _rope_tables(pos_b)
    shared = None
    for l in range(DEPTH):
        g0, b0 = ln_g[l, 0].reshape(1, D), ln_b[l, 0].reshape(1, D)
        g1, b1 = ln_g[l, 1].reshape(1, D), ln_b[l, 1].reshape(1, D)
        if l < N_A:
            xf, xt = _retention_layer(xf, xb, ret_w_in[l], ret_w_out[l], tabs, g0, b0, B, S)
        else:
            if shared is None:
                shared = _mla_shared(xb, mla_w_dkv, mla_kv_norm, mla_w_ukv, tabs)
            j = l - N_A
            xf, xt = _mla_layer(xf, xb, shared, mla_w_dq[j], mla_q_norm[j], mla_w_uq[j], mla_w_o[j],
                                tabs, g0, b0, B, S)
        xf, xb = _peer_layer(xf, xt, peer_w_q[l], peer_sub_keys[l], peer_u[l], peer_v[l], g1, b1)
    return xf.reshape(B, S, D)
```
